```python
import jax, jax.numpy as jnp
from jax import lax
import numpy as np

D_MODEL = 4096
BATCH = 1
SEQ = 8192
DEPTH = 1
DEC_BATCH = 16
DEC_SEQ = 64
PAST_LEN = 1024

CHUNK = 64
EPS = 1e-6
MIX_DIM = D_MODEL
V_HEAD_DIM = 128
MLA_HEADS = (MIX_DIM // 2) // V_HEAD_DIM
QK_NOPE = 128
ROPE_DIM = 64
QK_DIM = QK_NOPE + ROPE_DIM
Q_LORA = 1024
KV_LORA = 512
ROPE_THETA = 10000.0
SOFTMAX_SCALE = QK_DIM ** -0.5
Q_BLOCK = 128
POOL_DIM = MIX_DIM - MLA_HEADS * V_HEAD_DIM
POOL_WINDOWS = (2, 4, 8, 16)
POOL_GROUPS = len(POOL_WINDOWS)
POOL_GC = POOL_DIM // POOL_GROUPS
POOL_HIST = max(POOL_WINDOWS) - 1
IN_DIM = Q_LORA + KV_LORA + ROPE_DIM + POOL_DIM
OFF_KV = Q_LORA
OFF_PE = Q_LORA + KV_LORA
OFF_POOL = Q_LORA + KV_LORA + ROPE_DIM
PEER_HEADS = 8
PEER_N_KEYS = 128
PEER_EXPERTS = PEER_N_KEYS * PEER_N_KEYS
PEER_KEY_DIM = 256
PEER_HALF = PEER_KEY_DIM // 2
PEER_TOPK = 16
PEER_BLOCK = 32

kernel_name = "hybrid_mla_pool_peer_stream_step"


def rmsnorm(x, g):
    xf = x.astype(jnp.float32)
    r = lax.rsqrt(jnp.mean(xf * xf, axis=-1, keepdims=True) + EPS)
    return (xf * r * g.astype(jnp.float32)).astype(x.dtype)


def rope(x, pos):
    half = ROPE_DIM // 2
    inv = ROPE_THETA ** (-2.0 * jnp.arange(half, dtype=jnp.float32) / ROPE_DIM)
    ang = pos.astype(jnp.float32)[:, None] * inv[None, :]
    shape = (1, pos.shape[0]) + (1,) * (x.ndim - 3) + (half,)
    cos = jnp.cos(ang).reshape(shape)
    sin = jnp.sin(ang).reshape(shape)
    xf = x.astype(jnp.float32)
    x1, x2 = xf[..., :half], xf[..., half:]
    return jnp.concatenate([x1 * cos - x2 * sin, x1 * sin + x2 * cos], axis=-1).astype(x.dtype)


def chunk_attn(q_nope, q_pe, k_nope, k_pe, v, q_pos, k_pos):
    s = (jnp.einsum('bqhd,bkhd->bhqk', q_nope, k_nope)
         + jnp.einsum('bqhr,bkr->bhqk', q_pe, k_pe)).astype(jnp.float32) * SOFTMAX_SCALE
    mask = (q_pos[:, None] // CHUNK) >= (k_pos[None, :] // CHUNK)
    s = jnp.where(mask[None, None], s, -jnp.inf)
    p = jax.nn.softmax(s, axis=-1)
    return jnp.einsum('bhqk,bkhd->bqhd', p.astype(v.dtype), v)


def mla_attention(q_nope, q_pe, k_nope, k_pe, v, q_pos, k_pos):
    B, T = q_nope.shape[0], q_nope.shape[1]
    if T > Q_BLOCK and T % Q_BLOCK == 0:
        nblk = T // Q_BLOCK
        qn = q_nope.reshape(B, nblk, Q_BLOCK, MLA_HEADS, QK_NOPE).transpose(1, 0, 2, 3, 4)
        qp = q_pe.reshape(B, nblk, Q_BLOCK, MLA_HEADS, ROPE_DIM).transpose(1, 0, 2, 3, 4)
        pb = q_pos.reshape(nblk, Q_BLOCK)
        o = lax.map(lambda a: chunk_attn(a[0], a[1], k_nope, k_pe, v, a[2], k_pos), (qn, qp, pb))
        o = o.transpose(1, 0, 2, 3, 4)
    else:
        o = chunk_attn(q_nope, q_pe, k_nope, k_pe, v, q_pos, k_pos)
    return o.reshape(B, T, MLA_HEADS * V_HEAD_DIM)


def pool_mixer(u_ext, abs_pos, T, pool_w, pool_scale):
    B, L, _ = u_ext.shape
    uf = u_ext.astype(jnp.float32)
    cs = jnp.cumsum(uf, axis=1)
    parts = []
    for g, w in enumerate(POOL_WINDOWS):
        c = cs[..., g * POOL_GC:(g + 1) * POOL_GC]
        lagged = jnp.pad(c, ((0, 0), (w, 0), (0, 0)))[:, :L]
        cnt = jnp.minimum(abs_pos + 1, w).astype(jnp.float32)[None, :, None]
        mean = (c - lagged) / cnt
        parts.append(mean[:, L - T:] - uf[:, L - T:, g * POOL_GC:(g + 1) * POOL_GC])
    d = jnp.stack(parts, axis=2)
    out = jnp.einsum('btgc,gce->btge', d, pool_w.astype(jnp.float32)).reshape(B, T, POOL_DIM)
    return (out * pool_scale.astype(jnp.float32)).astype(u_ext.dtype)


def peer_ffn(h, wq, sk1, sk2, u, v):
    B, T, D = h.shape
    q = jnp.einsum('btd,dhk->bthk', h, wq)
    s1 = jnp.einsum('bthk,hnk->bthn', q[..., :PEER_HALF], sk1).astype(jnp.float32)
    s2 = jnp.einsum('bthk,hnk->bthn', q[..., PEER_HALF:], sk2).astype(jnp.float32)
    v1, i1 = lax.top_k(s1, PEER_TOPK)
    v2, i2 = lax.top_k(s2, PEER_TOPK)
    nc = PEER_TOPK * PEER_TOPK
    cand = (v1[..., :, None] + v2[..., None, :]).reshape(B, T, PEER_HEADS, nc)
    cidx = (i1[..., :, None] * PEER_N_KEYS + i2[..., None, :]).reshape(B, T, PEER_HEADS, nc)
    top, sel = lax.top_k(cand, PEER_TOPK)
    idx = jnp.take_along_axis(cidx, sel, axis=-1)
    gate = jax.nn.softmax(top, axis=-1)
    n = B * T
    K = PEER_HEADS * PEER_TOPK
    nb = -(-n // PEER_BLOCK)
    pad = nb * PEER_BLOCK - n
    hf = jnp.pad(h.reshape(n, D), ((0, pad), (0, 0))).reshape(nb, PEER_BLOCK, D)
    idf = jnp.pad(idx.reshape(n, K), ((0, pad), (0, 0))).reshape(nb, PEER_BLOCK, K)
    gf = jnp.pad(gate.reshape(n, K), ((0, pad), (0, 0))).reshape(nb, PEER_BLOCK, K)

    def expert_block(args):
        hb, ib, gb = args
        a = jnp.einsum('nd,nkd->nk', hb, jnp.take(u, ib, axis=0)).astype(jnp.float32)
        act = (jax.nn.gelu(a, approximate=False) * gb).astype(hb.dtype)
        return jnp.einsum('nk,nkd->nd', act, jnp.take(v, ib, axis=0))

    y = lax.map(expert_block, (hf, idf, gf))
    return y.reshape(nb * PEER_BLOCK, D)[:n].reshape(B, T, D).astype(h.dtype)


def hybrid_layer(x, pos, ckv_hist, kpe_hist, pool_hist, ln1_g, w_in, q_norm_g, w_uq, kv_norm_g,
                 w_ukv, pool_w, pool_scale, w_o, ln2_g, peer_wq, peer_sk1, peer_sk2, peer_u, peer_v):
    B, T, _ = x.shape
    h = rmsnorm(x, ln1_g)
    z = jnp.einsum('btd,de->bte', h, w_in)
    zq, zkv = z[..., :OFF_KV], z[..., OFF_KV:OFF_PE]
    zpe, zpool = z[..., OFF_PE:OFF_POOL], z[..., OFF_POOL:]
    q = jnp.einsum('btc,che->bthe', rmsnorm(zq, q_norm_g), w_uq)
    q_nope, q_pe = q[..., :QK_NOPE], rope(q[..., QK_NOPE:], pos)
    ckv_new = rmsnorm(zkv, kv_norm_g)
    kpe_new = rope(zpe, pos)
    if ckv_hist is None:
        ckv_all, kpe_all, k_pos = ckv_new, kpe_new, pos
    else:
        ckv_all = jnp.concatenate([ckv_hist.astype(ckv_new.dtype), ckv_new], axis=1)
        kpe_all = jnp.concatenate([kpe_hist.astype(kpe_new.dtype), kpe_new], axis=1)
        k_pos = jnp.arange(ckv_all.shape[1], dtype=jnp.int32)
    kv = jnp.einsum('bsc,che->bshe', ckv_all, w_ukv)
    k_nope, v = kv[..., :QK_NOPE], kv[..., QK_NOPE:]
    o_mla = mla_attention(q_nope, q_pe, k_nope, kpe_all, v, pos, k_pos)
    if pool_hist is None:
        u_ext, abs_pos = zpool, pos
    else:
        u_ext = jnp.concatenate([pool_hist.astype(zpool.dtype), zpool], axis=1)
        abs_pos = pos[0] - POOL_HIST + jnp.arange(POOL_HIST + T, dtype=jnp.int32)
    o_pool = pool_mixer(u_ext, abs_pos, T, pool_w, pool_scale)
    pool_new = u_ext[:, -POOL_HIST:]
    o = jnp.concatenate([o_mla.astype(x.dtype), o_pool.astype(x.dtype)], axis=-1)
    x = x + jnp.einsum('btm,md->btd', o, w_o).astype(x.dtype)
    x = x + peer_ffn(rmsnorm(x, ln2_g), peer_wq, peer_sk1, peer_sk2, peer_u, peer_v)
    return x, ckv_new, kpe_new, pool_new


def setup_inputs(seed: int = 0) -> dict:
    key = jax.random.key(seed)
    ks = jax.random.split(key, 24)

    def nrm(k, shape, scale):
        return jax.random.normal(k, shape, jnp.float32) * scale

    return {
        "x_prompt": nrm(ks[0], (BATCH, SEQ, D_MODEL), 1.0),
        "x_sample": nrm(ks[1], (DEC_BATCH, DEC_SEQ, D_MODEL), 1.0),
        "cache_ckv": nrm(ks[2], (DEPTH, DEC_BATCH, PAST_LEN, KV_LORA), 1.0),
        "cache_kpe": nrm(ks[3], (DEPTH, DEC_BATCH, PAST_LEN, ROPE_DIM), 1.0),
        "state_pool": nrm(ks[4], (DEPTH, DEC_BATCH, POOL_HIST, POOL_DIM), 1.0),
        "ln1_g": 1.0 + nrm(ks[5], (DEPTH, D_MODEL), 0.02),
        "w_in": nrm(ks[6], (DEPTH, D_MODEL, IN_DIM), D_MODEL ** -0.5),
        "q_norm_g": 1.0 + nrm(ks[7], (DEPTH, Q_LORA), 0.02),
        "w_uq": nrm(ks[8], (DEPTH, Q_LORA, MLA_HEADS, QK_DIM), Q_LORA ** -0.5),
        "kv_norm_g": 1.0 + nrm(ks[9], (DEPTH, KV_LORA), 0.02),
        "w_ukv": nrm(ks[10], (DEPTH, KV_LORA, MLA_HEADS, QK_NOPE + V_HEAD_DIM), KV_LORA ** -0.5),
        "pool_w": nrm(ks[11], (DEPTH, POOL_GROUPS, POOL_GC, POOL_GC), POOL_GC ** -0.5),
        "pool_scale": 1.0 + nrm(ks[12], (DEPTH, POOL_DIM), 0.02),
        "w_o": nrm(ks[13], (DEPTH, MIX_DIM, D_MODEL), MIX_DIM ** -0.5),
        "ln2_g": 1.0 + nrm(ks[14], (DEPTH, D_MODEL), 0.02),
        "peer_wq": nrm(ks[15], (DEPTH, D_MODEL, PEER_HEADS, PEER_KEY_DIM), D_MODEL ** -0.5),
        "peer_sk1": nrm(ks[16], (DEPTH, PEER_HEADS, PEER_N_KEYS, PEER_HALF), PEER_HALF ** -0.5),
        "peer_sk2": nrm(ks[17], (DEPTH, PEER_HEADS, PEER_N_KEYS, PEER_HALF), PEER_HALF ** -0.5),
        "peer_u": nrm(ks[18], (DEPTH, PEER_EXPERTS, D_MODEL), D_MODEL ** -0.5),
        "peer_v": nrm(ks[19], (DEPTH, PEER_EXPERTS, D_MODEL), (PEER_HEADS * PEER_TOPK) ** -0.5),
        "final_g": 1.0 + nrm(ks[20], (D_MODEL,), 0.02),
    }


def reference(x_prompt, x_sample, cache_ckv, cache_kpe, state_pool, ln1_g, w_in, q_norm_g, w_uq,
              kv_norm_g, w_ukv, pool_w, pool_scale, w_o, ln2_g, peer_wq, peer_sk1, peer_sk2,
              peer_u, peer_v, final_g):
    xp, xs = x_prompt, x_sample
    past = cache_ckv.shape[2]
    pos_p = jnp.arange(xp.shape[1], dtype=jnp.int32)
    pos_s = past + jnp.arange(xs.shape[1], dtype=jnp.int32)
    ckv_p, kpe_p, pool_p, ckv_s, kpe_s, pool_s = [], [], [], [], [], []
    for l in range(DEPTH):
        w = (ln1_g[l], w_in[l], q_norm_g[l], w_uq[l], kv_norm_g[l], w_ukv[l], pool_w[l],
             pool_scale[l], w_o[l], ln2_g[l], peer_wq[l], peer_sk1[l], peer_sk2[l],
             peer_u[l], peer_v[l])
        xp, c1, k1, p1 = hybrid_layer(xp, pos_p, None, None, None, *w)
        xs, c2, k2, p2 = hybrid_layer(xs, pos_s, cache_ckv[l], cache_kpe[l], state_pool[l], *w)
        ckv_p.append(c1); kpe_p.append(k1); pool_p.append(p1)
        ckv_s.append(c2); kpe_s.append(k2); pool_s.append(p2)
    y_prompt = rmsnorm(xp, final_g)
    y_sample = rmsnorm(xs, final_g)
    return (y_prompt, y_sample, jnp.stack(ckv_p), jnp.stack(kpe_p), jnp.stack(pool_p),
            jnp.stack(ckv_s), jnp.stack(kpe_s), jnp.stack(pool_s))
```

```python
import functools
import math

import jax
import jax.numpy as jnp
from jax import lax
from jax.experimental import pallas as pl
from jax.experimental.pallas import tpu as pltpu

F32 = jnp.float32
BF16 = jnp.bfloat16

CHUNK = 64
CHUNK_SHIFT = 6
EPS = 1e-6
V_HEAD_DIM = 128
QK_NOPE = 128
ROPE_DIM = 64
QK_DIM = QK_NOPE + ROPE_DIM
ROPE_THETA = 10000.0
SOFTMAX_SCALE = QK_DIM ** -0.5
POOL_WINDOWS = (2, 4, 8, 16)
POOL_HIST = max(POOL_WINDOWS) - 1
HALO = POOL_HIST + 1
PEER_TOPK = 16

LANES = 128
V7X_VMEM_LIMIT_BYTES = 58 * 1024 * 1024


def _cparams(semantics):
    return pltpu.CompilerParams(dimension_semantics=semantics, vmem_limit_bytes=V7X_VMEM_LIMIT_BYTES)


def _rms(x, g):
    r = lax.rsqrt(jnp.mean(x * x, axis=-1, keepdims=True) + EPS)
    return x * r * g


def _rms_two_source_kernel(xa_ref, xb_ref, g_ref, o_ref, *, na_blocks):
    i = pl.program_id(0)

    @pl.when(i < na_blocks)
    def _():
        o_ref[...] = _rms(xa_ref[...], g_ref[...]).astype(o_ref.dtype)

    @pl.when(i >= na_blocks)
    def _():
        o_ref[...] = _rms(xb_ref[...], g_ref[...]).astype(o_ref.dtype)


def rms_two_source(xa, xb, g, tm):
    na, d = xa.shape
    nb = xb.shape[0]
    na_blocks, nb_blocks = na // tm, nb // tm
    return pl.pallas_call(
        functools.partial(_rms_two_source_kernel, na_blocks=na_blocks),
        out_shape=jax.ShapeDtypeStruct((na + nb, d), BF16),
        grid=(na_blocks + nb_blocks,),
        in_specs=[
            pl.BlockSpec((tm, d), lambda i: (jnp.minimum(i, na_blocks - 1), 0)),
            pl.BlockSpec((tm, d), lambda i: (jnp.maximum(i - na_blocks, 0), 0)),
            pl.BlockSpec((1, d), lambda i: (0, 0)),
        ],
        out_specs=pl.BlockSpec((tm, d), lambda i: (i, 0)),
        compiler_params=_cparams(("arbitrary",)),
        name="rms_in",
    )(xa, xb, g.reshape(1, d))


def _rms_cols_kernel(x_ref, g_ref, *o_refs):
    y = _rms(x_ref[...], g_ref[...])
    for o_ref in o_refs:
        o_ref[...] = y.astype(o_ref.dtype)


def rms_cols(z, col_block, width, g, out_dtypes, tm):
    n = z.shape[0]
    return pl.pallas_call(
        _rms_cols_kernel,
        out_shape=[jax.ShapeDtypeStruct((n, width), dt) for dt in out_dtypes],
        grid=(n // tm,),
        in_specs=[
            pl.BlockSpec((tm, width), lambda i: (i, col_block)),
            pl.BlockSpec((1, width), lambda i: (0, 0)),
        ],
        out_specs=[pl.BlockSpec((tm, width), lambda i: (i, 0)) for _ in out_dtypes],
        compiler_params=_cparams(("arbitrary",)),
        name="rms_cols",
    )(z, g.reshape(1, width))


def _rms_transposed_kernel(x_ref, g_ref, o_ref):
    o_ref[...] = _rms(x_ref[...], g_ref[...]).T.astype(o_ref.dtype)


def rms_transposed(x, g, tm):
    n, d = x.shape
    return pl.pallas_call(
        _rms_transposed_kernel,
        out_shape=jax.ShapeDtypeStruct((d, n), BF16),
        grid=(n // tm,),
        in_specs=[pl.BlockSpec((tm, d), lambda i: (i, 0)), pl.BlockSpec((1, d), lambda i: (0, 0))],
        out_specs=pl.BlockSpec((d, tm), lambda i: (0, i)),
        compiler_params=_cparams(("arbitrary",)),
        name="rms_transposed",
    )(x, g.reshape(1, d))


def _add_rms_kernel(x_ref, y_ref, g_ref, o_ref):
    o_ref[...] = _rms(x_ref[...] + y_ref[...], g_ref[...])


def add_rms(x, y, g, row_block0, n_rows, tm):
    d = x.shape[1]
    return pl.pallas_call(
        _add_rms_kernel,
        out_shape=jax.ShapeDtypeStruct((n_rows, d), F32),
        grid=(n_rows // tm,),
        in_specs=[
            pl.BlockSpec((tm, d), lambda i: (i + row_block0, 0)),
            pl.BlockSpec((tm, d), lambda i: (i + row_block0, 0)),
            pl.BlockSpec((1, d), lambda i: (0, 0)),
        ],
        out_specs=pl.BlockSpec((tm, d), lambda i: (i, 0)),
        compiler_params=_cparams(("arbitrary",)),
        name="add_rms_out",
    )(x, y, g.reshape(1, d))


def _matmul_kernel(a_ref, b_ref, o_ref):
    o_ref[...] = jnp.dot(a_ref[...], b_ref[...], preferred_element_type=F32).astype(o_ref.dtype)


def matmul(a, b, out_dtype, tm, tn):
    m, k = a.shape
    n = b.shape[1]
    return pl.pallas_call(
        _matmul_kernel,
        out_shape=jax.ShapeDtypeStruct((m, n), out_dtype),
        grid=(m // tm, n // tn),
        in_specs=[pl.BlockSpec((tm, k), lambda i, j: (i, 0)), pl.BlockSpec((k, tn), lambda i, j: (0, j))],
        out_specs=pl.BlockSpec((tm, tn), lambda i, j: (i, j)),
        compiler_params=_cparams(("arbitrary", "arbitrary")),
        name="matmul",
    )(a, b)


def _out_proj_kernel(a0_ref, a1_ref, b0_ref, b1_ref, xa_ref, xb_ref, o_ref, *, na_blocks):
    i = pl.program_id(0)
    acc = jnp.dot(a0_ref[...], b0_ref[...], preferred_element_type=F32)
    acc = acc + jnp.dot(a1_ref[...], b1_ref[...], preferred_element_type=F32)

    @pl.when(i < na_blocks)
    def _():
        o_ref[...] = xa_ref[...] + acc

    @pl.when(i >= na_blocks)
    def _():
        o_ref[...] = xb_ref[...] + acc


def out_proj(o_mla, o_pool, w_o, xa, xb, tm, tn):
    n, half = o_mla.shape
    d = w_o.shape[1]
    na_blocks = xa.shape[0] // tm
    return pl.pallas_call(
        functools.partial(_out_proj_kernel, na_blocks=na_blocks),
        out_shape=jax.ShapeDtypeStruct((n, d), F32),
        grid=(n // tm, d // tn),
        in_specs=[
            pl.BlockSpec((tm, half), lambda i, j: (i, 0)),
            pl.BlockSpec((tm, half), lambda i, j: (i, 0)),
            pl.BlockSpec((half, tn), lambda i, j: (0, j)),
            pl.BlockSpec((half, tn), lambda i, j: (1, j)),
            pl.BlockSpec((tm, tn), lambda i, j: (jnp.minimum(i, na_blocks - 1), j)),
            pl.BlockSpec((tm, tn), lambda i, j: (jnp.maximum(i - na_blocks, 0), j)),
        ],
        out_specs=pl.BlockSpec((tm, tn), lambda i, j: (i, j)),
        compiler_params=_cparams(("arbitrary", "arbitrary")),
        name="out_proj",
    )(o_mla, o_pool, w_o, w_o, xa, xb)


def _head_matmul_kernel(a_ref, b_ref, o_ref):
    o_ref[0] = jnp.dot(a_ref[...], b_ref[0], preferred_element_type=F32).astype(o_ref.dtype)


def head_matmul(a, b, tm):
    n, k = a.shape
    heads, _, w = b.shape
    return pl.pallas_call(
        _head_matmul_kernel,
        out_shape=jax.ShapeDtypeStruct((heads, n, w), BF16),
        grid=(n // tm, heads),
        in_specs=[pl.BlockSpec((tm, k), lambda i, h: (i, 0)), pl.BlockSpec((1, k, w), lambda i, h: (h, 0, 0))],
        out_specs=pl.BlockSpec((1, tm, w), lambda i, h: (h, i, 0)),
        compiler_params=_cparams(("arbitrary", "arbitrary")),
        name="kv_proj",
    )(a, b)


def _rope_half(a, cos2_ref, sin2_ref):
    return a * cos2_ref[...] + pltpu.roll(a, ROPE_DIM, axis=1) * sin2_ref[...]


def _q_proj_kernel(a_ref, b_ref, cos2_ref, sin2_ref, o_ref):
    acc = jnp.dot(a_ref[...], b_ref[0], preferred_element_type=F32)
    o_ref[0, :, :QK_NOPE] = acc[:, :QK_NOPE].astype(o_ref.dtype)
    roped = _rope_half(acc[:, QK_NOPE:], cos2_ref, sin2_ref)
    o_ref[0, :, QK_NOPE:] = roped[:, :ROPE_DIM].astype(o_ref.dtype)


def q_proj(qn, w_uq_ext, cos2, sin2, tm):
    n, k = qn.shape
    heads = w_uq_ext.shape[0]
    return pl.pallas_call(
        _q_proj_kernel,
        out_shape=jax.ShapeDtypeStruct((heads, n, QK_DIM), BF16),
        grid=(n // tm, heads),
        in_specs=[
            pl.BlockSpec((tm, k), lambda i, h: (i, 0)),
            pl.BlockSpec((1, k, 2 * LANES), lambda i, h: (h, 0, 0)),
            pl.BlockSpec((tm, LANES), lambda i, h: (i, 0)),
            pl.BlockSpec((tm, LANES), lambda i, h: (i, 0)),
        ],
        out_specs=pl.BlockSpec((1, tm, QK_DIM), lambda i, h: (h, i, 0)),
        compiler_params=_cparams(("arbitrary", "arbitrary")),
        name="q_proj",
    )(qn, w_uq_ext, cos2, sin2)


def _kpe_rope_kernel(z_ref, cos2_ref, sin2_ref, o32_ref, o16_ref):
    roped = _rope_half(z_ref[...], cos2_ref, sin2_ref)[:, :ROPE_DIM]
    o32_ref[...] = roped
    o16_ref[...] = roped.astype(o16_ref.dtype)


def kpe_rope(z, col_block, cos2, sin2, tm):
    n = z.shape[0]
    return pl.pallas_call(
        _kpe_rope_kernel,
        out_shape=[jax.ShapeDtypeStruct((n, ROPE_DIM), F32), jax.ShapeDtypeStruct((n, ROPE_DIM), BF16)],
        grid=(n // tm,),
        in_specs=[
            pl.BlockSpec((tm, LANES), lambda i: (i, col_block)),
            pl.BlockSpec((tm, LANES), lambda i: (i, 0)),
            pl.BlockSpec((tm, LANES), lambda i: (i, 0)),
        ],
        out_specs=[pl.BlockSpec((tm, ROPE_DIM), lambda i: (i, 0)), pl.BlockSpec((tm, ROPE_DIM), lambda i: (i, 0))],
        compiler_params=_cparams(("arbitrary",)),
        name="kpe_rope",
    )(z, cos2, sin2)


def _chunk_mask(rows, cols, row_pos0, col_pos0):
    r = lax.broadcasted_iota(jnp.int32, (rows, cols), 0) + row_pos0
    c = lax.broadcasted_iota(jnp.int32, (rows, cols), 1) + col_pos0
    return (r >> CHUNK_SHIFT) >= (c >> CHUNK_SHIFT)


def _prompt_attn_kernel(q_ref, kv_ref, kpe_ref, o_ref, kcat_ref, m_ref, l_ref, acc_ref, *, tq):
    qi = pl.program_id(1)

    @pl.when(qi == 0)
    def _():
        kcat_ref[:, :QK_NOPE] = kv_ref[0, :, :QK_NOPE]
        kcat_ref[:, QK_NOPE:] = kpe_ref[...]

    q = q_ref[0]
    m_ref[...] = jnp.full(m_ref.shape, -jnp.inf, F32)
    l_ref[...] = jnp.zeros(l_ref.shape, F32)
    acc_ref[...] = jnp.zeros(acc_ref.shape, F32)

    def block(kj, masked):
        start = pl.multiple_of(kj * tq, tq)
        k = kcat_ref[pl.ds(start, tq), :]
        v = kv_ref[0, pl.ds(start, tq), QK_NOPE:]
        s = lax.dot_general(q, k, (((1,), (1,)), ((), ())), preferred_element_type=F32) * SOFTMAX_SCALE
        if masked:
            s = jnp.where(_chunk_mask(tq, tq, 0, 0), s, -jnp.inf)
        m_old = m_ref[...]
        m_new = jnp.maximum(m_old, jnp.max(s, axis=-1, keepdims=True))
        alpha = jnp.exp(m_old - m_new)
        p = jnp.exp(s - m_new)
        l_ref[...] = alpha * l_ref[...] + jnp.sum(p, axis=-1, keepdims=True)
        acc_ref[...] = alpha * acc_ref[...] + jnp.dot(p.astype(BF16), v, preferred_element_type=F32)
        m_ref[...] = m_new

    def body(kj, carry):
        block(kj, masked=False)
        return carry

    lax.fori_loop(0, qi, body, 0)
    block(qi, masked=True)
    o_ref[...] = (acc_ref[...] / l_ref[...]).astype(o_ref.dtype)


def prompt_attention(q, kv, kpe, t, tq):
    heads = q.shape[0]
    return pl.pallas_call(
        functools.partial(_prompt_attn_kernel, tq=tq),
        out_shape=jax.ShapeDtypeStruct((t, heads * V_HEAD_DIM), BF16),
        grid=(heads, t // tq),
        in_specs=[
            pl.BlockSpec((1, tq, QK_DIM), lambda h, i: (h, i, 0)),
            pl.BlockSpec((1, t, QK_NOPE + V_HEAD_DIM), lambda h, i: (h, 0, 0)),
            pl.BlockSpec((t, ROPE_DIM), lambda h, i: (0, 0)),
        ],
        out_specs=pl.BlockSpec((tq, V_HEAD_DIM), lambda h, i: (i, h)),
        scratch_shapes=[
            pltpu.VMEM((t, QK_DIM), BF16),
            pltpu.VMEM((tq, 1), F32),
            pltpu.VMEM((tq, 1), F32),
            pltpu.VMEM((tq, V_HEAD_DIM), F32),
        ],
        compiler_params=_cparams(("arbitrary", "arbitrary")),
        name="prompt_attn",
    )(q, kv, kpe)


def _sample_attn_kernel(q_ref, kv_ref, kpe_ref, o_ref, *, past):
    q = q_ref[0]
    tq = q.shape[0]
    s_len = kv_ref.shape[1]
    nt = (((1,), (1,)), ((), ()))
    s = lax.dot_general(q[:, :QK_NOPE], kv_ref[0, :, :QK_NOPE], nt, preferred_element_type=F32)
    s = s + lax.dot_general(q[:, QK_NOPE:], kpe_ref[0], nt, preferred_element_type=F32)
    s = jnp.where(_chunk_mask(tq, s_len, past, 0), s * SOFTMAX_SCALE, -jnp.inf)
    p = jnp.exp(s - jnp.max(s, axis=-1, keepdims=True))
    l = jnp.sum(p, axis=-1, keepdims=True)
    o = jnp.dot(p.astype(BF16), kv_ref[0, :, QK_NOPE:], preferred_element_type=F32)
    o_ref[...] = (o / l).astype(o_ref.dtype)


def sample_attention(q, row0, kv, kpe, dec_batch, dec_seq, past):
    heads = q.shape[0]
    s_len = past + dec_seq
    blk0 = row0 // dec_seq
    return pl.pallas_call(
        functools.partial(_sample_attn_kernel, past=past),
        out_shape=jax.ShapeDtypeStruct((dec_batch * dec_seq, heads * V_HEAD_DIM), BF16),
        grid=(dec_batch, heads),
        in_specs=[
            pl.BlockSpec((1, dec_seq, QK_DIM), lambda b, h: (h, blk0 + b, 0)),
            pl.BlockSpec((1, s_len, QK_NOPE + V_HEAD_DIM), lambda b, h: (h, b, 0)),
            pl.BlockSpec((1, s_len, ROPE_DIM), lambda b, h: (b, 0, 0)),
        ],
        out_specs=pl.BlockSpec((dec_seq, V_HEAD_DIM), lambda b, h: (b, h)),
        compiler_params=_cparams(("arbitrary", "arbitrary")),
        name="sample_attn",
    )(q, kv, kpe)


def _pool_kernel(z_ref, halo_ref, w_ref, scale_ref, o_ref, ext_ref, *, window_of_group, pos0, zero_first):
    i = pl.program_id(0)
    g = pl.program_id(1)
    tm = z_ref.shape[0]
    halo = halo_ref[...].reshape(HALO, halo_ref.shape[-1])
    if zero_first:
        halo = jnp.where(i == 0, 0.0, halo)
    u = z_ref[...]
    ext_ref[:HALO] = halo
    ext_ref[HALO:] = u
    pos = lax.broadcasted_iota(jnp.int32, (tm, 1), 0) + (i * tm if zero_first else 0) + pos0

    for gi, w in enumerate(window_of_group):

        @pl.when(g == gi)
        def _(w=w):
            acc = u
            for k in range(1, w):
                acc = acc + ext_ref[pl.ds(HALO - k, tm), :]
            cnt = jnp.minimum(pos + 1, w).astype(F32)
            d = (acc / cnt - u).astype(BF16)
            out = jnp.dot(d, w_ref[0], preferred_element_type=F32) * scale_ref[...]
            o_ref[...] = out.astype(o_ref.dtype)


def pool_mixer(z, col_block0, halo, halo_index_map, pool_w, pool_scale, row_block0, n_rows, tm, pos0, zero_first):
    groups, gc, _ = pool_w.shape
    halo_block = (HALO, gc) if halo.ndim == 2 else (1, HALO, gc)
    return pl.pallas_call(
        functools.partial(_pool_kernel, window_of_group=POOL_WINDOWS, pos0=pos0, zero_first=zero_first),
        out_shape=jax.ShapeDtypeStruct((n_rows, groups * gc), BF16),
        grid=(n_rows // tm, groups),
        in_specs=[
            pl.BlockSpec((tm, gc), lambda i, g: (i + row_block0, col_block0 + g)),
            pl.BlockSpec(halo_block, halo_index_map),
            pl.BlockSpec((1, gc, gc), lambda i, g: (g, 0, 0)),
            pl.BlockSpec((1, gc), lambda i, g: (0, g)),
        ],
        out_specs=pl.BlockSpec((tm, gc), lambda i, g: (i, g)),
        scratch_shapes=[pltpu.VMEM((tm + HALO, gc), F32)],
        compiler_params=_cparams(("arbitrary", "arbitrary")),
        name="pool_mixer",
    )(z, halo, pool_w, pool_scale.reshape(1, groups * gc))


def _top_values(s, k):
    vals = []
    work = s
    for _ in range(k):
        m = jnp.max(work, axis=0, keepdims=True)
        vals.append(m)
        work = jnp.where(work >= m, -jnp.inf, work)
    return vals


def _peer_route_kernel(q_ref, sk1_ref, sk2_ref, s1_ref, s2_ref, tau_ref, b_ref):
    heads = q_ref.shape[0]

    def head(h, carry):
        s1 = jnp.dot(sk1_ref[h], q_ref[h, 0], preferred_element_type=F32)
        s2 = jnp.dot(sk2_ref[h], q_ref[h, 1], preferred_element_type=F32)
        s1_ref[h] = s1
        s2_ref[h] = s2
        v1 = _top_values(s1, PEER_TOPK)
        v2_rows = _top_values(s2, PEER_TOPK)
        row = lax.broadcasted_iota(jnp.int32, (PEER_TOPK, s2.shape[1]), 0)
        v2 = jnp.zeros((PEER_TOPK, s2.shape[1]), F32)
        for k in range(PEER_TOPK):
            v2 = jnp.where(row == k, v2_rows[k], v2)
        cands = [v1[a] + v2 for a in range(PEER_TOPK)]
        cmax = v1[0] + v2_rows[0]
        work = list(cands)
        tau = cmax
        for _ in range(PEER_TOPK):
            m = work[0]
            for a in range(1, PEER_TOPK):
                m = jnp.maximum(m, work[a])
            tau = jnp.max(m, axis=0, keepdims=True)
            work = [jnp.where(w >= tau, -jnp.inf, w) for w in work]
        z = jnp.zeros_like(tau)
        for c in cands:
            z = z + jnp.sum(jnp.where(c >= tau, jnp.exp(c - cmax), 0.0), axis=0, keepdims=True)
        tau_ref[pl.ds(h, 1), :] = tau
        b_ref[pl.ds(h, 1), :] = cmax + jnp.log(z)
        return carry

    lax.fori_loop(0, heads, head, 0)


def peer_route(q_t, sk1, sk2, tc):
    heads, _, half, n = q_t.shape
    keys = sk1.shape[1]
    return pl.pallas_call(
        _peer_route_kernel,
        out_shape=[
            jax.ShapeDtypeStruct((heads, keys, n), F32),
            jax.ShapeDtypeStruct((heads, keys, n), F32),
            jax.ShapeDtypeStruct((heads, n), F32),
            jax.ShapeDtypeStruct((heads, n), F32),
        ],
        grid=(n // tc,),
        in_specs=[
            pl.BlockSpec((heads, 2, half, tc), lambda i: (0, 0, 0, i)),
            pl.BlockSpec((heads, keys, half), lambda i: (0, 0, 0)),
            pl.BlockSpec((heads, keys, half), lambda i: (0, 0, 0)),
        ],
        out_specs=[
            pl.BlockSpec((heads, keys, tc), lambda i: (0, 0, i)),
            pl.BlockSpec((heads, keys, tc), lambda i: (0, 0, i)),
            pl.BlockSpec((heads, tc), lambda i: (0, i)),
            pl.BlockSpec((heads, tc), lambda i: (0, i)),
        ],
        compiler_params=_cparams(("arbitrary",)),
        name="peer_route",
    )(q_t, sk1, sk2)


_SQRT_HALF = math.sqrt(0.5)


def _gelu(a):
    return 0.5 * a * (1.0 + lax.erf(a * _SQRT_HALF))


def _peer_expert_kernel(u_ref, v_ref, h_ref, s1_ref, s2_ref, tau_ref, b_ref, o_ref, a_ref, act_ref):
    e = pl.program_id(1)
    te, tm = a_ref.shape
    heads, keys, _ = s2_ref.shape
    a_ref[...] = jnp.dot(u_ref[...], h_ref[...], preferred_element_type=F32)

    for r in range(te // keys):
        for t in range(tm // LANES):
            lanes = slice(t * LANES, (t + 1) * LANES)
            gate = jnp.zeros((keys, LANES), F32)
            for h in range(heads):
                c = s2_ref[h, :, lanes] + s1_ref[r, h : h + 1, lanes]
                sel = c >= tau_ref[h : h + 1, lanes]
                gate = gate + jnp.where(sel, jnp.exp(c - b_ref[h : h + 1, lanes]), 0.0)
            rows = slice(r * keys, (r + 1) * keys)
            act_ref[rows, lanes] = (_gelu(a_ref[rows, lanes]) * gate).astype(act_ref.dtype)

    @pl.when(e == 0)
    def _():
        o_ref[...] = jnp.zeros(o_ref.shape, F32)

    act = act_ref[...]
    tn = tm
    for j in range(o_ref.shape[1] // tn):
        cols = slice(j * tn, (j + 1) * tn)
        o_ref[:, cols] += lax.dot_general(act, v_ref[:, cols], (((0,), (0,)), ((), ())),
                                          preferred_element_type=F32)


def peer_experts(u, v, h_t, s1, s2, tau, b, tm, te):
    n_exp, d = u.shape
    n = h_t.shape[1]
    heads, keys, _ = s2.shape
    s1_km = s1.transpose(1, 0, 2)
    return pl.pallas_call(
        _peer_expert_kernel,
        out_shape=jax.ShapeDtypeStruct((n, d), F32),
        grid=(n // tm, n_exp // te),
        in_specs=[
            pl.BlockSpec((te, d), lambda i, e: (e, 0)),
            pl.BlockSpec((te, d), lambda i, e: (e, 0)),
            pl.BlockSpec((d, tm), lambda i, e: (0, i)),
            pl.BlockSpec((te // keys, heads, tm), lambda i, e: (e, 0, i)),
            pl.BlockSpec((heads, keys, tm), lambda i, e: (0, 0, i)),
            pl.BlockSpec((heads, tm), lambda i, e: (0, i)),
            pl.BlockSpec((heads, tm), lambda i, e: (0, i)),
        ],
        out_specs=pl.BlockSpec((tm, d), lambda i, e: (i, 0)),
        scratch_shapes=[pltpu.VMEM((te, tm), F32), pltpu.VMEM((te, tm), BF16)],
        compiler_params=_cparams(("arbitrary", "arbitrary")),
        name="peer_experts",
    )(u, v, h_t, s1_km, s2, tau, b)


def _rope_tables(pos):
    half = ROPE_DIM // 2
    inv = ROPE_THETA ** (-2.0 * jnp.arange(half, dtype=F32) / ROPE_DIM)
    ang = pos.astype(F32)[:, None] * inv[None, :]
    cos, sin = jnp.cos(ang), jnp.sin(ang)
    zeros = jnp.zeros((pos.shape[0], ROPE_DIM), F32)
    return jnp.concatenate([cos, cos, zeros], axis=1), jnp.concatenate([-sin, sin, zeros], axis=1)


def _swap_halves(w):
    half = w.shape[-1] // 2
    return jnp.concatenate([w[..., half:], w[..., :half]], axis=-1)


def _layer(xp, xs, cache_ckv, cache_kpe, state_pool, ln1_g, w_in, q_norm_g, w_uq, kv_norm_g, w_ukv, pool_w,
           pool_scale, w_o, ln2_g, peer_wq, peer_sk1, peer_sk2, peer_u, peer_v, final_g, *, tm, tq, tc, te):
    t, d = xp.shape
    dec_batch, dec_seq, _ = xs.shape
    past = cache_ckv.shape[1]
    ns = dec_batch * dec_seq
    n = t + ns
    xs2 = xs.reshape(ns, d)
    q_lora = q_norm_g.shape[0]
    kv_lora = kv_norm_g.shape[0]
    heads = w_uq.shape[1]
    groups, gc, _ = pool_w.shape
    pool_dim = groups * gc
    off_kv, off_pe, off_pool = q_lora, q_lora + kv_lora, q_lora + kv_lora + ROPE_DIM

    w_pe = w_in[:, off_pe:off_pool]
    tn_in = 768
    in_cols = off_pe + pool_dim + 2 * ROPE_DIM
    in_pad = -in_cols % tn_in
    w_in_ext = jnp.concatenate(
        [w_in[:, :off_pe], w_in[:, off_pool:], w_pe, _swap_halves(w_pe), jnp.zeros((d, in_pad), F32)], axis=1
    ).astype(BF16)
    col_pe = (off_pe + pool_dim) // LANES
    w_uq_h = w_uq.transpose(1, 0, 2)
    w_uq_ext = jnp.concatenate([w_uq_h, _swap_halves(w_uq_h[..., QK_NOPE:])], axis=-1).astype(BF16)
    w_ukv_h = w_ukv.transpose(1, 0, 2).astype(BF16)
    pool_w16 = pool_w.astype(BF16)
    w_o16 = w_o.astype(BF16)
    wq_t = peer_wq.reshape(d, -1).T.astype(BF16)
    sk1 = peer_sk1.astype(BF16)
    sk2 = peer_sk2.astype(BF16)
    u16 = peer_u.astype(BF16)
    v16 = peer_v.astype(BF16)

    pos = jnp.concatenate([jnp.arange(t, dtype=jnp.int32),
                           jnp.tile(past + jnp.arange(dec_seq, dtype=jnp.int32), dec_batch)])
    cos2, sin2 = _rope_tables(pos)

    h1 = rms_two_source(xp, xs2, ln1_g, tm)
    z = matmul(h1, w_in_ext, F32, tm, tn_in)

    (qn,) = rms_cols(z, 0, q_lora, q_norm_g, (BF16,), tm)
    q = q_proj(qn, w_uq_ext, cos2, sin2, tm)
    ckv32, ckv16 = rms_cols(z, off_kv // kv_lora, kv_lora, kv_norm_g, (F32, BF16), tm)
    kpe32, kpe16 = kpe_rope(z, col_pe, cos2, sin2, tm)

    kv_p = head_matmul(ckv16[:t], w_ukv_h, tm)
    o_p = prompt_attention(q, kv_p, kpe16[:t], t, tq)

    ckv_all = jnp.concatenate([cache_ckv.astype(BF16), ckv16[t:].reshape(dec_batch, dec_seq, kv_lora)], axis=1)
    kpe_all = jnp.concatenate([cache_kpe.astype(BF16), kpe16[t:].reshape(dec_batch, dec_seq, ROPE_DIM)], axis=1)
    s_len = past + dec_seq
    kv_s = head_matmul(ckv_all.reshape(dec_batch * s_len, kv_lora), w_ukv_h, s_len)
    o_s = sample_attention(q, t, kv_s, kpe_all, dec_batch, dec_seq, past)
    o_mla = jnp.concatenate([o_p, o_s], axis=0)

    col_pool = off_pe // gc
    halo_rows = tm // HALO
    pool_p = pool_mixer(z, col_pool, z, lambda i, g: (jnp.maximum(i * halo_rows - 1, 0), col_pool + g),
                        pool_w16, pool_scale, 0, t, tm, 0, True)
    hist = jnp.concatenate([jnp.zeros((dec_batch, 1, pool_dim), F32), state_pool], axis=1)
    pool_s = pool_mixer(z, col_pool, hist, lambda i, g: (i, 0, g),
                        pool_w16, pool_scale, t // dec_seq, ns, dec_seq, past, False)
    o_pool = jnp.concatenate([pool_p, pool_s], axis=0)

    x1 = out_proj(o_mla, o_pool, w_o16, xp, xs2, tm, tm)

    h2_t = rms_transposed(x1, ln2_g, tm)
    q_t = matmul(wq_t, h2_t, BF16, tm, tm)
    peer_heads, keys, half = peer_sk1.shape
    s1, s2, tau, b = peer_route(q_t.reshape(peer_heads, 2, half, n), sk1, sk2, tc)
    y_peer = peer_experts(u16, v16, h2_t, s1, s2, tau, b, tm, te)

    y_p = add_rms(x1, y_peer, final_g, 0, t, tm)
    y_s = add_rms(x1, y_peer, final_g, t // tm, ns, tm)

    pool_cols = z[:, off_pe:off_pe + pool_dim]
    new_pool_p = pool_cols[t - POOL_HIST:t]
    new_pool_s = pool_cols[t:].reshape(dec_batch, dec_seq, pool_dim)[:, dec_seq - POOL_HIST:]
    return (y_p, y_s.reshape(dec_batch, dec_seq, d), ckv32[:t], kpe32[:t], new_pool_p,
            ckv32[t:].reshape(dec_batch, dec_seq, kv_lora), kpe32[t:].reshape(dec_batch, dec_seq, ROPE_DIM),
            new_pool_s)


def kernel(x_prompt, x_sample, cache_ckv, cache_kpe, state_pool, ln1_g, w_in, q_norm_g, w_uq, kv_norm_g, w_ukv,
           pool_w, pool_scale, w_o, ln2_g, peer_wq, peer_sk1, peer_sk2, peer_u, peer_v, final_g):
    assert x_prompt.shape[0] == 1 and ln1_g.shape[0] == 1, "one prompt stream and one layer are supported"
    outs = _layer(x_prompt[0], x_sample, cache_ckv[0], cache_kpe[0], state_pool[0], ln1_g[0], w_in[0],
                  q_norm_g[0], w_uq[0], kv_norm_g[0], w_ukv[0], pool_w[0], pool_scale[0], w_o[0], ln2_g[0],
                  peer_wq[0], peer_sk1[0], peer_sk2[0], peer_u[0], peer_v[0], final_g,
                  tm=512, tq=512, tc=128, te=512)
    y_p, y_s, ckv_p, kpe_p, pool_p, ckv_s, kpe_s, pool_s = outs
    return (y_p[None], y_s, ckv_p[None, None], kpe_p[None, None], pool_p[None, None],
            ckv_s[None], kpe_s[None], pool_s[None])
```
